```python
import jax, jax.numpy as jnp
from jax import lax
import numpy as np

D_MODEL = 1024
BATCH = 2
SEQ = 8192
DEPTH = 1

D_MIX = D_MODEL
GLA_HEADS = 4
GLA_VDIM = D_MIX // 2
GLA_DV = GLA_VDIM // GLA_HEADS
GLA_DK = GLA_DV // 2
GLA_KDIM = GLA_HEADS * GLA_DK
GLA_GATE_RANK = 16
GLA_GATE_NORM = 16.0
GLA_CHUNK = 64
POOL_WIDTH = D_MIX - GLA_VDIM
POOL_WINDOWS = (2, 4, 8, 16)
POOL_GROUPS = 4
POOL_GW = POOL_WIDTH // POOL_GROUPS
IN_SIZES = (GLA_KDIM, GLA_KDIM, GLA_VDIM, GLA_VDIM, POOL_WIDTH, GLA_GATE_RANK)
IN_COLS = int(sum(IN_SIZES))
IN_SPLITS = [int(v) for v in np.cumsum(IN_SIZES)[:-1]]
PEER_HEADS = 8
PEER_NKEYS = 128
PEER_NEXPERTS = PEER_NKEYS * PEER_NKEYS
PEER_HALF = 128
PEER_QDIM = 2 * PEER_HALF
PEER_TOPK = 16
PEER_TOK_BLOCK = 128
EPS = 1e-6

kernel_name = 'hymba_gla_pool_peer_adaln_block'


def rmsnorm(x, gain):
    xf = x.astype(jnp.float32)
    y = xf * lax.rsqrt(jnp.mean(xf * xf, axis=-1, keepdims=True) + EPS) * gain.astype(jnp.float32)
    return y.astype(x.dtype)


def modulate(h, shift, scale):
    return h * (1.0 + scale[:, None, :]) + shift[:, None, :]


def gla_chunked(q, k, v, log_a):
    B, S, H, DK = q.shape
    DV = v.shape[-1]
    C = GLA_CHUNK
    N = S // C
    def to_chunks(t):
        return t.astype(jnp.float32).reshape(B, N, C, H, t.shape[-1]).transpose(0, 3, 1, 2, 4)
    q, k, v, log_a = to_chunks(q), to_chunks(k), to_chunks(v), to_chunks(log_a)
    q = q * (DK ** -0.5)
    b = jnp.cumsum(log_a, axis=3)
    b_last = b[:, :, :, -1:, :]
    q_in = q * jnp.exp(b)
    k_in = k * jnp.exp(-b)
    k_st = k * jnp.exp(b_last - b)
    causal = jnp.tril(jnp.ones((C, C), dtype=bool))
    attn = jnp.where(causal, jnp.einsum('bhnid,bhnjd->bhnij', q_in, k_in), 0.0)
    o_intra = jnp.einsum('bhnij,bhnjv->bhniv', attn, v)
    kv = jnp.einsum('bhncd,bhncv->nbhdv', k_st, v)
    decay = jnp.exp(b_last[:, :, :, 0, :]).transpose(2, 0, 1, 3)
    def step(state, inp):
        kv_n, dec_n = inp
        return state * dec_n[..., None] + kv_n, state
    s0 = jnp.zeros((B, H, DK, DV), jnp.float32)
    _, states = lax.scan(step, s0, (kv, decay))
    o_inter = jnp.einsum('bhncd,nbhdv->bhncv', q_in, states)
    o = o_intra + o_inter
    return o.transpose(0, 2, 3, 1, 4).reshape(B, S, H, DV)


def causal_multiscale_pool(p, pool_w, pool_scale):
    B, S, _ = p.shape
    pg = p.astype(jnp.float32).reshape(B, S, POOL_GROUPS, POOL_GW)
    pos = jnp.arange(S)
    outs = []
    for gi, w in enumerate(POOL_WINDOWS):
        xg = pg[:, :, gi]
        cs = jnp.cumsum(xg, axis=1)
        lag = jnp.pad(cs, ((0, 0), (w, 0), (0, 0)))[:, :S]
        count = jnp.minimum(pos + 1, w).astype(jnp.float32)
        outs.append((cs - lag) / count[None, :, None] - xg)
    pooled = jnp.stack(outs, axis=2)
    y = jnp.einsum('bsgc,gcd->bsgd', pooled, pool_w.astype(jnp.float32)).reshape(B, S, POOL_WIDTH)
    return (y * pool_scale.astype(jnp.float32)).astype(p.dtype)


def token_mixer(h, w_in, gla_gate_w, gla_gate_b, gla_norm_g, pool_w, pool_scale, w_out):
    B, S, _ = h.shape
    proj = h @ w_in
    q, k, v, g, p, a_lr = jnp.split(proj, IN_SPLITS, axis=-1)
    log_a = jax.nn.log_sigmoid((a_lr @ gla_gate_w + gla_gate_b).astype(jnp.float32)) / GLA_GATE_NORM
    o = gla_chunked(q.reshape(B, S, GLA_HEADS, GLA_DK), k.reshape(B, S, GLA_HEADS, GLA_DK),
                    v.reshape(B, S, GLA_HEADS, GLA_DV), log_a.reshape(B, S, GLA_HEADS, GLA_DK))
    o = rmsnorm(o, gla_norm_g).astype(h.dtype) * jax.nn.silu(g.reshape(B, S, GLA_HEADS, GLA_DV))
    y_gla = o.reshape(B, S, GLA_VDIM)
    y_pool = causal_multiscale_pool(p, pool_w, pool_scale).astype(h.dtype)
    return jnp.concatenate([y_gla, y_pool], axis=-1) @ w_out


def peer_ffn(h, w_q, sub_k1, sub_k2, expert_u, expert_v):
    B, S, D = h.shape
    T = B * S
    ht = h.reshape(T, D)
    qp = (ht @ w_q).reshape(T, PEER_HEADS, 2, PEER_HALF)
    s1 = jnp.einsum('thk,nk->thn', qp[:, :, 0], sub_k1).astype(jnp.float32)
    s2 = jnp.einsum('thk,nk->thn', qp[:, :, 1], sub_k2).astype(jnp.float32)
    v1, i1 = lax.top_k(s1, PEER_TOPK)
    v2, i2 = lax.top_k(s2, PEER_TOPK)
    cand_s = (v1[..., :, None] + v2[..., None, :]).reshape(T, PEER_HEADS, PEER_TOPK * PEER_TOPK)
    cand_i = (i1[..., :, None] * PEER_NKEYS + i2[..., None, :]).reshape(T, PEER_HEADS, PEER_TOPK * PEER_TOPK)
    top_s, pos = lax.top_k(cand_s, PEER_TOPK)
    idx = jnp.take_along_axis(cand_i, pos, axis=-1)
    gates = jax.nn.softmax(top_s, axis=-1).astype(h.dtype)
    NBLK = T // PEER_TOK_BLOCK
    E = PEER_HEADS * PEER_TOPK
    hb = ht.reshape(NBLK, PEER_TOK_BLOCK, D)
    ib = idx.reshape(NBLK, PEER_TOK_BLOCK, E)
    gb = gates.reshape(NBLK, PEER_TOK_BLOCK, E)
    def expert_block(args):
        xb, eb, wb = args
        act = jax.nn.gelu(jnp.einsum('td,ted->te', xb, expert_u[eb]), approximate=False) * wb
        return jnp.einsum('te,ted->td', act, expert_v[eb])
    out = lax.map(expert_block, (hb, ib, gb))
    return out.reshape(B, S, D)


def setup_inputs(seed: int = 0) -> dict:
    key = jax.random.key(seed)
    ks = jax.random.split(key, 20)
    f32 = jnp.float32
    nrm = lambda k, shape, s: jax.random.normal(k, shape, f32) * s
    L, D = DEPTH, D_MODEL
    return {
        'x': nrm(ks[0], (BATCH, SEQ, D), 1.0),
        'c': nrm(ks[1], (BATCH, D), 1.0),
        'ada_w': nrm(ks[2], (L, D, 6 * D), 0.5 * D ** -0.5),
        'ada_b': nrm(ks[3], (L, 6 * D), 0.02),
        'norm1_g': 1.0 + nrm(ks[4], (L, D), 0.02),
        'w_in': nrm(ks[5], (L, D, IN_COLS), D ** -0.5),
        'gla_gate_w': nrm(ks[6], (L, GLA_GATE_RANK, GLA_KDIM), GLA_GATE_RANK ** -0.5),
        'gla_gate_b': nrm(ks[7], (L, GLA_KDIM), 0.02),
        'gla_norm_g': 1.0 + nrm(ks[8], (L, GLA_DV), 0.02),
        'pool_w': nrm(ks[9], (L, POOL_GROUPS, POOL_GW, POOL_GW), POOL_GW ** -0.5),
        'pool_scale': 1.0 + nrm(ks[10], (L, POOL_WIDTH), 0.02),
        'w_out': nrm(ks[11], (L, D_MIX, D), D_MIX ** -0.5),
        'norm2_g': 1.0 + nrm(ks[12], (L, D), 0.02),
        'peer_wq': nrm(ks[13], (L, D, PEER_HEADS * PEER_QDIM), D ** -0.5),
        'peer_k1': nrm(ks[14], (L, PEER_NKEYS, PEER_HALF), PEER_HALF ** -0.5),
        'peer_k2': nrm(ks[15], (L, PEER_NKEYS, PEER_HALF), PEER_HALF ** -0.5),
        'peer_u': nrm(ks[16], (L, PEER_NEXPERTS, D), D ** -0.5),
        'peer_v': nrm(ks[17], (L, PEER_NEXPERTS, D), PEER_HEADS ** -0.5),
        'final_g': 1.0 + nrm(ks[18], (D,), 0.02),
    }


def reference(x, c, ada_w, ada_b, norm1_g, w_in, gla_gate_w, gla_gate_b, gla_norm_g, pool_w, pool_scale,
              w_out, norm2_g, peer_wq, peer_k1, peer_k2, peer_u, peer_v, final_g):
    for l in range(DEPTH):
        mod = jax.nn.silu(c) @ ada_w[l] + ada_b[l]
        shift1, scale1, gate1, shift2, scale2, gate2 = jnp.split(mod, 6, axis=-1)
        h = modulate(rmsnorm(x, norm1_g[l]), shift1, scale1)
        mix = token_mixer(h, w_in[l], gla_gate_w[l], gla_gate_b[l], gla_norm_g[l], pool_w[l], pool_scale[l], w_out[l])
        x = x + gate1[:, None, :] * mix
        h = modulate(rmsnorm(x, norm2_g[l]), shift2, scale2)
        ffn = peer_ffn(h, peer_wq[l], peer_k1[l], peer_k2[l], peer_u[l], peer_v[l])
        x = x + gate2[:, None, :] * ffn
    return rmsnorm(x, final_g)
```

```python
import functools
import math

import jax
import jax.numpy as jnp
from jax import lax
from jax.experimental import pallas as pl
from jax.experimental.pallas import tpu as pltpu

F32 = jnp.float32
BF16 = jnp.bfloat16
HIGHEST = lax.Precision.HIGHEST

EPS = 1e-6
GLA_HEADS = 4
GLA_DK = 64
GLA_DV = 128
GLA_KDIM = GLA_HEADS * GLA_DK
GLA_VDIM = GLA_HEADS * GLA_DV
GLA_GATE_RANK = 16
GLA_GATE_NORM = 16.0
GLA_CHUNK = 64
POOL_WINDOWS = (2, 4, 8, 16)
POOL_GW = 128
POOL_WIDTH = POOL_GW * len(POOL_WINDOWS)
POOL_HALO = 16
PEER_HEADS = 8
PEER_NKEYS = 128
PEER_HALF = 128
PEER_TOPK = 16
OFF_Q, OFF_K, OFF_V, OFF_G, OFF_P = 0, 256, 512, 1024, 1536
MAIN_COLS = 2048

VMEM_LIMIT = 56 * 1024 * 1024

INPROJ_TM = 512
MIX_L = 256
ROUTE_TM = 256
EXP_TM = 512
EXP_TE = 512

NEG_INF = float("-inf")
POS_INF = float("inf")


def _rms(x, gain):
    return x * lax.rsqrt(jnp.mean(x * x, axis=-1, keepdims=True) + EPS) * gain


def _ada_kernel(c_ref, w_ref, b_ref, o_ref):
    c = c_ref[...]
    o_ref[...] = jnp.dot(jax.nn.silu(c), w_ref[...], preferred_element_type=F32,
                         precision=HIGHEST) + b_ref[...]


def _ada(c, ada_w, ada_b):
    B, D = c.shape
    n = ada_w.shape[1]
    bn = 1024
    return pl.pallas_call(
        _ada_kernel,
        grid=(n // bn,),
        in_specs=[pl.BlockSpec((B, D), lambda j: (0, 0)),
                  pl.BlockSpec((D, bn), lambda j: (0, j)),
                  pl.BlockSpec((1, bn), lambda j: (0, j))],
        out_specs=pl.BlockSpec((B, bn), lambda j: (0, j)),
        out_shape=jax.ShapeDtypeStruct((B, n), F32),
        compiler_params=pltpu.CompilerParams(dimension_semantics=("arbitrary",),
                                             vmem_limit_bytes=VMEM_LIMIT),
        name="ada",
    )(c, ada_w, ada_b.reshape(1, n))


def _inproj_kernel(x_ref, mod_ref, g1_ref, wm_ref, wa_ref, gw_ref, gb_ref, p_ref, la_ref):
    x = x_ref[...]
    shift = mod_ref[0:1, :]
    scale = mod_ref[1:2, :]
    h = _rms(x, g1_ref[...]) * (1.0 + scale) + shift
    hb = h.astype(BF16)
    p_ref[...] = jnp.dot(hb, wm_ref[...], preferred_element_type=F32)
    alr = jnp.dot(hb, wa_ref[...], preferred_element_type=F32)
    z = jnp.dot(alr, gw_ref[...], preferred_element_type=F32, precision=HIGHEST) + gb_ref[...]
    la_ref[...] = (jnp.minimum(z, 0.0) - jnp.log1p(jnp.exp(-jnp.abs(z)))) / GLA_GATE_NORM


def _inproj(x, mod, g1, w_main, w_a, gate_w, gate_b):
    B, S, D = x.shape
    tm = INPROJ_TM
    return pl.pallas_call(
        _inproj_kernel,
        grid=(B, S // tm),
        in_specs=[pl.BlockSpec((None, tm, D), lambda b, i: (b, i, 0)),
                  pl.BlockSpec((None, 6, D), lambda b, i: (b, 0, 0)),
                  pl.BlockSpec((1, D), lambda b, i: (0, 0)),
                  pl.BlockSpec((D, MAIN_COLS), lambda b, i: (0, 0)),
                  pl.BlockSpec((D, GLA_GATE_RANK), lambda b, i: (0, 0)),
                  pl.BlockSpec((GLA_GATE_RANK, GLA_KDIM), lambda b, i: (0, 0)),
                  pl.BlockSpec((1, GLA_KDIM), lambda b, i: (0, 0))],
        out_specs=[pl.BlockSpec((None, tm, MAIN_COLS), lambda b, i: (b, i, 0)),
                   pl.BlockSpec((None, tm, GLA_KDIM), lambda b, i: (b, i, 0))],
        out_shape=[jax.ShapeDtypeStruct((B, S, MAIN_COLS), F32),
                   jax.ShapeDtypeStruct((B, S, GLA_KDIM), F32)],
        compiler_params=pltpu.CompilerParams(dimension_semantics=("arbitrary", "arbitrary"),
                                             vmem_limit_bytes=VMEM_LIMIT),
        name="inproj",
    )(x, mod, g1, w_main, w_a, gate_w, gate_b)


def _mixer_kernel(p_ref, la_ref, x_ref, mod_ref, gn_ref, pw_ref, ps_ref, wo_ref, g2_ref,
                  x1_ref, h2_ref, state_ref, pext_ref, ycat_ref):
    L = p_ref.shape[0]
    n_chunks = L // GLA_CHUNK
    j = pl.program_id(1)

    @pl.when(j == 0)
    def _():
        state_ref[...] = jnp.zeros_like(state_ref)
        pext_ref[0:POOL_HALO, :] = jnp.zeros((POOL_HALO, POOL_WIDTH), F32)

    row = lax.broadcasted_iota(jnp.int32, (L, L), 0)
    col = lax.broadcasted_iota(jnp.int32, (L, L), 1)
    same_chunk = (row // GLA_CHUNK) == (col // GLA_CHUNK)
    causal = same_chunk & (col <= row)
    la = la_ref[...]
    bcum = jnp.dot(causal.astype(F32), la, preferred_element_type=F32, precision=HIGHEST)
    blast = jnp.dot(same_chunk.astype(F32), la, preferred_element_type=F32, precision=HIGHEST)
    q = p_ref[:, OFF_Q:OFF_Q + GLA_KDIM]
    k = p_ref[:, OFF_K:OFF_K + GLA_KDIM]
    q_in = (q * (GLA_DK ** -0.5)) * jnp.exp(bcum)
    k_in = k * jnp.exp(-bcum)
    k_st = k * jnp.exp(blast - bcum)
    decay = jnp.exp(blast)

    gn = gn_ref[...]
    for h in range(GLA_HEADS):
        ks = slice(h * GLA_DK, (h + 1) * GLA_DK)
        vs = slice(OFF_V + h * GLA_DV, OFF_V + (h + 1) * GLA_DV)
        gs = slice(OFF_G + h * GLA_DV, OFF_G + (h + 1) * GLA_DV)
        qh = q_in[:, ks].astype(BF16)
        v_h = p_ref[:, vs]
        vb = v_h.astype(BF16)
        att = lax.dot_general(qh, k_in[:, ks].astype(BF16), (((1,), (1,)), ((), ())),
                              preferred_element_type=F32)
        att = jnp.where(causal, att, 0.0)
        o = jnp.dot(att.astype(BF16), vb, preferred_element_type=F32)
        st = state_ref[h]
        inter = []
        for c in range(n_chunks):
            rs = slice(c * GLA_CHUNK, (c + 1) * GLA_CHUNK)
            inter.append(lax.dot_general(qh[rs], st.astype(BF16), (((1,), (1,)), ((), ())),
                                         preferred_element_type=F32))
            kv_t = jnp.dot(v_h[rs].T.astype(BF16), k_st[rs, ks].astype(BF16),
                           preferred_element_type=F32)
            st = st * decay[c * GLA_CHUNK:c * GLA_CHUNK + 1, ks] + kv_t
        state_ref[h] = st
        o = o + jnp.concatenate(inter, axis=0)
        o = _rms(o, gn)
        ycat_ref[:, h * GLA_DV:(h + 1) * GLA_DV] = (o * jax.nn.silu(p_ref[:, gs])).astype(BF16)

    pext_ref[POOL_HALO:POOL_HALO + L, :] = p_ref[:, OFF_P:OFF_P + POOL_WIDTH]
    pos = j * L + lax.broadcasted_iota(jnp.int32, (L, 1), 0)
    for gi, w in enumerate(POOL_WINDOWS):
        cs = slice(gi * POOL_GW, (gi + 1) * POOL_GW)
        xg = pext_ref[POOL_HALO:POOL_HALO + L, cs]
        acc = xg
        for d in range(1, w):
            acc = acc + pext_ref[POOL_HALO - d:POOL_HALO - d + L, cs]
        count = jnp.minimum(pos + 1, w).astype(F32)
        pooled = acc / count - xg
        y = jnp.dot(pooled.astype(BF16), pw_ref[gi], preferred_element_type=F32)
        ycat_ref[:, GLA_VDIM + gi * POOL_GW:GLA_VDIM + (gi + 1) * POOL_GW] = (
            y * ps_ref[:, cs]).astype(BF16)
    pext_ref[0:POOL_HALO, :] = pext_ref[L:L + POOL_HALO, :]

    mix = jnp.dot(ycat_ref[...], wo_ref[...], preferred_element_type=F32)
    x1 = x_ref[...] + mod_ref[2:3, :] * mix
    x1_ref[...] = x1
    h2 = _rms(x1, g2_ref[...]) * (1.0 + mod_ref[4:5, :]) + mod_ref[3:4, :]
    h2_ref[...] = h2.astype(BF16)


def _mixer(proj, la, x, mod, gn, pool_w, pool_scale, w_out, g2):
    B, S, D = x.shape
    L = MIX_L
    return pl.pallas_call(
        _mixer_kernel,
        grid=(B, S // L),
        in_specs=[pl.BlockSpec((None, L, MAIN_COLS), lambda b, i: (b, i, 0)),
                  pl.BlockSpec((None, L, GLA_KDIM), lambda b, i: (b, i, 0)),
                  pl.BlockSpec((None, L, D), lambda b, i: (b, i, 0)),
                  pl.BlockSpec((None, 6, D), lambda b, i: (b, 0, 0)),
                  pl.BlockSpec((1, GLA_DV), lambda b, i: (0, 0)),
                  pl.BlockSpec((len(POOL_WINDOWS), POOL_GW, POOL_GW), lambda b, i: (0, 0, 0)),
                  pl.BlockSpec((1, POOL_WIDTH), lambda b, i: (0, 0)),
                  pl.BlockSpec((D, D), lambda b, i: (0, 0)),
                  pl.BlockSpec((1, D), lambda b, i: (0, 0))],
        out_specs=[pl.BlockSpec((None, L, D), lambda b, i: (b, i, 0)),
                   pl.BlockSpec((None, L, D), lambda b, i: (b, i, 0))],
        out_shape=[jax.ShapeDtypeStruct((B, S, D), F32),
                   jax.ShapeDtypeStruct((B, S, D), BF16)],
        scratch_shapes=[pltpu.VMEM((GLA_HEADS, GLA_DV, GLA_DK), F32),
                        pltpu.VMEM((POOL_HALO + L, POOL_WIDTH), F32),
                        pltpu.VMEM((L, D), BF16)],
        compiler_params=pltpu.CompilerParams(dimension_semantics=("arbitrary", "arbitrary"),
                                             vmem_limit_bytes=VMEM_LIMIT),
        name="mixer",
    )(proj, la, x, mod, gn, pool_w, pool_scale, w_out, g2)


def _joint_candidates():
    return [(a, b) for a in range(PEER_TOPK) for b in range(PEER_TOPK)
            if (a + 1) * (b + 1) <= PEER_TOPK]


def _route_kernel(h2_ref, wq_ref, k1_ref, k2_ref, ht_ref, s2m_ref, e2_ref, thr_ref, w_ref,
                  s_scr, v_scr):
    tm = h2_ref.shape[0]
    h2 = h2_ref[...]
    ht_ref[...] = h2.astype(F32).T.astype(BF16)
    qp = jnp.dot(h2, wq_ref[...], preferred_element_type=F32)
    keys = (k1_ref[...], k2_ref[...])
    for h in range(PEER_HEADS):
        for half in range(2):
            c0 = (2 * h + half) * PEER_HALF
            s_t = lax.dot_general(keys[half], qp[:, c0:c0 + PEER_HALF].astype(BF16),
                                  (((1,), (1,)), ((), ())), preferred_element_type=F32)
            s_scr[half, h] = s_t
            work = s_t
            for a in range(PEER_TOPK):
                m = jnp.max(work, axis=0, keepdims=True)
                v_scr[half, a, h:h + 1, :] = m
                work = jnp.where(work == m, NEG_INF, work)

    v1 = [v_scr[0, a] for a in range(PEER_TOPK)]
    v2 = [v_scr[1, a] for a in range(PEER_TOPK)]
    cands = [v1[a] + v2[b] for a, b in _joint_candidates()]
    work = list(cands)
    tau = None
    for r in range(PEER_TOPK):
        tau = functools.reduce(jnp.maximum, work)
        if r + 1 < PEER_TOPK:
            work = [jnp.where(c == tau, NEG_INF, c) for c in work]
    top = v1[0] + v2[0]
    z = functools.reduce(jnp.add, [jnp.where(c >= tau, jnp.exp(c - top), 0.0) for c in cands])
    inv_z = 1.0 / z
    th1 = v1[PEER_TOPK - 1]
    th2 = v2[PEER_TOPK - 1]
    for h in range(PEER_HEADS):
        hs = slice(h, h + 1)
        s1 = s_scr[0, h]
        s2 = s_scr[1, h]
        keep2 = s2 >= th2[hs]
        s2m_ref[h] = jnp.where(keep2, s2, NEG_INF)
        e2_ref[h] = jnp.where(keep2, jnp.exp(s2 - v2[0][hs]), 0.0)
        keep1 = s1 >= th1[hs]
        thr_ref[h] = jnp.where(keep1, tau[hs] - s1, POS_INF)
        w_ref[h] = jnp.where(keep1, jnp.exp(s1 - v1[0][hs]) * inv_z[hs], 0.0)


def _route(h2, wq, k1, k2):
    T, D = h2.shape
    tm = ROUTE_TM
    big = jax.ShapeDtypeStruct((PEER_HEADS, PEER_NKEYS, T), F32)
    big_spec = pl.BlockSpec((PEER_HEADS, PEER_NKEYS, tm), lambda i: (0, 0, i))
    return pl.pallas_call(
        _route_kernel,
        grid=(T // tm,),
        in_specs=[pl.BlockSpec((tm, D), lambda i: (i, 0)),
                  pl.BlockSpec((D, PEER_HEADS * 2 * PEER_HALF), lambda i: (0, 0)),
                  pl.BlockSpec((PEER_NKEYS, PEER_HALF), lambda i: (0, 0)),
                  pl.BlockSpec((PEER_NKEYS, PEER_HALF), lambda i: (0, 0))],
        out_specs=[pl.BlockSpec((D, tm), lambda i: (0, i)), big_spec, big_spec, big_spec, big_spec],
        out_shape=[jax.ShapeDtypeStruct((D, T), BF16), big, big, big, big],
        scratch_shapes=[pltpu.VMEM((2, PEER_HEADS, PEER_NKEYS, tm), F32),
                        pltpu.VMEM((2, PEER_TOPK, PEER_HEADS, tm), F32)],
        compiler_params=pltpu.CompilerParams(dimension_semantics=("arbitrary",),
                                             vmem_limit_bytes=VMEM_LIMIT),
        name="route",
    )(h2, wq, k1, k2)


def _experts_kernel(u_ref, vt_ref, ht_ref, s2m_ref, e2_ref, thr_ref, w_ref, x1_ref, gate_ref, fg_ref,
                    out_ref, acc_ref, g_scr):
    te = u_ref.shape[0]
    rows_per_step = te // PEER_NKEYS
    j = pl.program_id(1)

    @pl.when(j == 0)
    def _():
        acc_ref[...] = jnp.zeros_like(acc_ref)

    a_t = jnp.dot(u_ref[...], ht_ref[...], preferred_element_type=F32)
    act = 0.5 * a_t * (1.0 + lax.erf(a_t * (1.0 / math.sqrt(2.0))))
    for r in range(rows_per_step):
        i1 = j * rows_per_step + r
        gate = None
        for h in range(PEER_HEADS):
            thr = thr_ref[h, pl.ds(i1, 1), :]
            wgt = w_ref[h, pl.ds(i1, 1), :]
            term = jnp.where(s2m_ref[h] >= thr, e2_ref[h], 0.0) * wgt
            gate = term if gate is None else gate + term
        rs = slice(r * PEER_NKEYS, (r + 1) * PEER_NKEYS)
        g_scr[rs, :] = (act[rs] * gate).astype(BF16)
    acc_ref[...] += jnp.dot(vt_ref[...], g_scr[...], preferred_element_type=F32)

    @pl.when(j == pl.num_programs(1) - 1)
    def _():
        ffn = acc_ref[...].T
        x2 = x1_ref[...] + gate_ref[...] * ffn
        out_ref[...] = _rms(x2, fg_ref[...])


def _experts(u, vt, ht, s2m, e2, thr, wgt, x1, gate2, final_g, seq_len):
    E, D = u.shape
    T = ht.shape[1]
    tm, te = EXP_TM, EXP_TE
    tiles_per_seq = seq_len // tm
    big_spec = pl.BlockSpec((PEER_HEADS, PEER_NKEYS, tm), lambda i, j: (0, 0, i))
    return pl.pallas_call(
        _experts_kernel,
        grid=(T // tm, E // te),
        in_specs=[pl.BlockSpec((te, D), lambda i, j: (j, 0)),
                  pl.BlockSpec((D, te), lambda i, j: (0, j)),
                  pl.BlockSpec((D, tm), lambda i, j: (0, i)),
                  big_spec, big_spec, big_spec, big_spec,
                  pl.BlockSpec((tm, D), lambda i, j: (i, 0)),
                  pl.BlockSpec((None, 1, D), lambda i, j: (i // tiles_per_seq, 0, 0)),
                  pl.BlockSpec((1, D), lambda i, j: (0, 0))],
        out_specs=pl.BlockSpec((tm, D), lambda i, j: (i, 0)),
        out_shape=jax.ShapeDtypeStruct((T, D), F32),
        scratch_shapes=[pltpu.VMEM((D, tm), F32),
                        pltpu.VMEM((te, tm), BF16)],
        compiler_params=pltpu.CompilerParams(dimension_semantics=("arbitrary", "arbitrary"),
                                             vmem_limit_bytes=VMEM_LIMIT),
        name="experts",
    )(u, vt, ht, s2m, e2, thr, wgt, x1, gate2, final_g)


def kernel(x, c, ada_w, ada_b, norm1_g, w_in, gla_gate_w, gla_gate_b, gla_norm_g, pool_w, pool_scale,
           w_out, norm2_g, peer_wq, peer_k1, peer_k2, peer_u, peer_v, final_g):
    B, S, D = x.shape
    assert ada_w.shape[0] == 1, "single-layer block: the final norm is fused into the expert kernel"
    l = 0
    mod = _ada(c, ada_w[l], ada_b[l]).reshape(B, 6, D)
    w_main = w_in[l][:, :MAIN_COLS].astype(BF16)
    w_a = w_in[l][:, MAIN_COLS:].astype(BF16)
    proj, la = _inproj(x, mod, norm1_g[l].reshape(1, D), w_main, w_a,
                       gla_gate_w[l], gla_gate_b[l].reshape(1, GLA_KDIM))
    x1, h2 = _mixer(proj, la, x, mod, gla_norm_g[l].reshape(1, GLA_DV), pool_w[l].astype(BF16),
                    pool_scale[l].reshape(1, POOL_WIDTH), w_out[l].astype(BF16),
                    norm2_g[l].reshape(1, D))
    ht, s2m, e2, thr, wgt = _route(h2.reshape(B * S, D), peer_wq[l].astype(BF16),
                                   peer_k1[l].astype(BF16), peer_k2[l].astype(BF16))
    y = _experts(peer_u[l].astype(BF16), peer_v[l].astype(BF16).T, ht, s2m, e2, thr, wgt,
                 x1.reshape(B * S, D), mod[:, 5:6, :], final_g.reshape(1, D), S)
    return y.reshape(B, S, D)
```
